```python
import jax, jax.numpy as jnp
from jax import lax
import numpy as np

D_MODEL = 1024
BATCH = 8
SEQ = 8192
DEPTH = 1
DEC_BATCH = 32
DEC_SEQ = 32
PAST_LEN = 2048

CHUNK = 64
MEM_LEN = 256
HG_HEADS = 4
HG_DIM = 128
HG_WIDTH = HG_HEADS * HG_DIM
FX_HEADS = 8
FX_DIM = 64
FX_WIDTH = FX_HEADS * FX_DIM
MIX_WIDTH = HG_WIDTH + FX_WIDTH
IN_COLS = 4 * HG_WIDTH + 3 * FX_WIDTH + FX_HEADS
Q_BLOCK = 128
XA_HEADS = 4
XA_DIM = D_MODEL // XA_HEADS
N_GROUPS = 4
EXPERTS_PER_GROUP = 8
N_EXPERTS = N_GROUPS * EXPERTS_PER_GROUP
TOP_K = 2
D_EXPERT = 512
ROW_BLOCK = 128
LN_EPS = 1e-5
RMS_EPS = 1e-6
DN_ALPHA = (2 * DEPTH) ** 0.25
DN_BETA = (8 * DEPTH) ** -0.25

kernel_name = 'hymba_style_hgrn2_fox_hmoe_stream_step'

F32 = jnp.float32


def layer_norm(x, g, b):
    xf = x.astype(F32)
    xc = xf - jnp.mean(xf, axis=-1, keepdims=True)
    var = jnp.mean(xc * xc, axis=-1, keepdims=True)
    y = xc * lax.rsqrt(var + LN_EPS) * g.astype(F32) + b.astype(F32)
    return y.astype(x.dtype)


def split_projection(h):
    bounds = np.cumsum([HG_WIDTH] * 4 + [FX_WIDTH] * 3).tolist()
    return jnp.split(h, bounds, axis=-1)


def hgrn2_inputs(zq, zf, zi, lb):
    bsz, L, _ = zq.shape
    def heads(t):
        return t.astype(F32).reshape(bsz, L, HG_HEADS, HG_DIM).transpose(0, 2, 1, 3)
    zf = zf.astype(F32)
    lb = lb.astype(F32)
    logf = jnp.log(lb + (1.0 - lb) * jax.nn.sigmoid(zf))
    k = (1.0 - lb) * jax.nn.sigmoid(-zf)
    return heads(zq), heads(k), heads(zi), heads(logf)


def hgrn2_chunk(S0, q, k, v, logf):
    L = q.shape[2]
    b = jnp.cumsum(logf, axis=2)
    causal = jnp.tril(jnp.ones((L, L), dtype=bool))
    diff = b[:, :, :, None, :] - b[:, :, None, :, :]
    decay = jnp.exp(jnp.where(causal[:, :, None], diff, -jnp.inf))
    scores = jnp.einsum('bhtd,bhsd,bhtsd->bhts', q, k, decay)
    o = (jnp.einsum('bhtd,bhde->bhte', q * jnp.exp(b), S0)
         + jnp.einsum('bhts,bhse->bhte', scores, v))
    b_end = b[:, :, -1]
    S_new = (jnp.exp(b_end)[..., None] * S0
             + jnp.einsum('bhsd,bhse->bhde', k * jnp.exp(b_end[:, :, None] - b), v))
    return S_new, o


def hgrn2_prompt(q, k, v, logf):
    bsz, H, S, d = q.shape
    nc = S // CHUNK
    def chunks(t):
        return jnp.moveaxis(t.reshape(bsz, H, nc, CHUNK, t.shape[-1]), 2, 0)
    S0 = jnp.zeros((bsz, H, HG_DIM, HG_DIM), F32)
    S_fin, o = lax.scan(lambda s, c: hgrn2_chunk(s, *c), S0,
                        (chunks(q), chunks(k), chunks(v), chunks(logf)))
    return S_fin, jnp.moveaxis(o, 0, 2).reshape(bsz, H, S, d)


def hgrn2_readout(o, zg, gain):
    bsz, H, L, d = o.shape
    o = o.transpose(0, 2, 1, 3)
    o = o * lax.rsqrt(jnp.mean(o * o, axis=-1, keepdims=True) + RMS_EPS) * gain.astype(F32).reshape(H, d)
    y = o.reshape(bsz, L, H * d) * jax.nn.silu(zg.astype(F32))
    return y.astype(zg.dtype)


def fox_inputs(zq, zk, zv, zf, fbias):
    bsz, L, _ = zq.shape
    def heads(t):
        return t.reshape(bsz, L, FX_HEADS, FX_DIM)
    logf = jax.nn.log_sigmoid(zf.astype(F32) + fbias.astype(F32))
    return heads(zq), heads(zk), heads(zv), logf


def fox_attend(q, k, v, dq, dk, q_pos, k_pos):
    s = jnp.einsum('bqhd,bkhd->bhqk', q, k).astype(F32) * (FX_DIM ** -0.5)
    s = s + dq.transpose(0, 2, 1)[..., None] - dk.transpose(0, 2, 1)[:, :, None, :]
    s = jnp.where(k_pos[None, :] <= q_pos[:, None], s, -jnp.inf)
    p = jax.nn.softmax(s, axis=-1)
    return jnp.einsum('bhqk,bkhd->bqhd', p.astype(v.dtype), v)


def fox_prompt(q, k, v, logf):
    bsz, S, H, d = q.shape
    D = jnp.cumsum(logf, axis=1)
    k_pos = jnp.arange(S)
    def block(i):
        start = i * Q_BLOCK
        qb = lax.dynamic_slice_in_dim(q, start, Q_BLOCK, axis=1)
        db = lax.dynamic_slice_in_dim(D, start, Q_BLOCK, axis=1)
        return fox_attend(qb, k, v, db, D, start + jnp.arange(Q_BLOCK), k_pos)
    o = lax.map(block, jnp.arange(S // Q_BLOCK))
    return jnp.moveaxis(o, 0, 1).reshape(bsz, S, H * d)


def fox_sample(q, k, v, logf, ck, cv, clogf):
    bsz, L, H, d = q.shape
    P = ck.shape[1]
    k_all = jnp.concatenate([ck.astype(k.dtype), k], axis=1)
    v_all = jnp.concatenate([cv.astype(v.dtype), v], axis=1)
    D = jnp.cumsum(jnp.concatenate([clogf.astype(F32), logf], axis=1), axis=1)
    o = fox_attend(q, k_all, v_all, D[:, P:], D, P + jnp.arange(L), jnp.arange(P + L))
    return o.reshape(bsz, L, H * d)


def memory_kv(mem, wk, wv):
    bsz, M, _ = mem.shape
    return ((mem @ wk).reshape(bsz, M, XA_HEADS, XA_DIM),
            (mem @ wv).reshape(bsz, M, XA_HEADS, XA_DIM))


def memory_attend(x, mk, mv, wq, wo):
    bsz, L, _ = x.shape
    q = (x @ wq).reshape(bsz, L, XA_HEADS, XA_DIM)
    s = jnp.einsum('blhd,bmhd->bhlm', q, mk.astype(q.dtype)).astype(F32) * (XA_DIM ** -0.5)
    p = jax.nn.softmax(s, axis=-1)
    o = jnp.einsum('bhlm,bmhd->blhd', p.astype(x.dtype), mv.astype(x.dtype))
    return o.reshape(bsz, L, D_MODEL) @ wo


def hier_moe(x, p):
    bsz, L, D = x.shape
    T = bsz * L
    xt = x.reshape(T, D)
    rows = jnp.arange(T)
    lg = (xt @ p['w_group']).astype(F32) + p['b_group'].astype(F32)
    pg = jax.nn.softmax(lg, axis=-1)
    g_sel = jnp.argmax(lg, axis=-1).astype(jnp.int32)
    le = ((xt @ p['w_expert']).astype(F32) + p['b_expert'].astype(F32)).reshape(T, N_GROUPS, EXPERTS_PER_GROUP)
    top_v, top_i = lax.top_k(le[rows, g_sel], TOP_K)
    gate = jax.nn.softmax(top_v, axis=-1) * pg[rows, g_sel][:, None]
    eid = g_sel[:, None] * EXPERTS_PER_GROUP + top_i.astype(jnp.int32)
    A = T * TOP_K
    e_flat = eid.reshape(A)
    tok_flat = jnp.repeat(jnp.arange(T, dtype=jnp.int32), TOP_K)
    gate_flat = gate.reshape(A)
    order = jnp.argsort(e_flat)
    e_s, tok_s, gate_s = e_flat[order], tok_flat[order], gate_flat[order]
    counts = jnp.bincount(e_flat, length=N_EXPERTS)
    starts = jnp.cumsum(counts) - counts
    padded = (counts + ROW_BLOCK - 1) // ROW_BLOCK * ROW_BLOCK
    pend = jnp.cumsum(padded)
    pstart = pend - padded
    dest = pstart[e_s] + (jnp.arange(A) - starts[e_s])
    n_blocks = -(-A // ROW_BLOCK) + N_EXPERTS
    P = n_blocks * ROW_BLOCK
    tok_pad = jnp.zeros((P,), jnp.int32).at[dest].set(tok_s)
    gate_pad = jnp.zeros((P,), F32).at[dest].set(gate_s)
    blk_e = jnp.minimum(jnp.searchsorted(pend, jnp.arange(n_blocks) * ROW_BLOCK, side='right'),
                        N_EXPERTS - 1)
    xb = xt[tok_pad].reshape(n_blocks, ROW_BLOCK, D)
    w_gate, w_up, w_down = p['w_gate'], p['w_up'], p['w_down']
    def expert_block(args):
        xr, e = args
        h = jax.nn.silu(xr @ w_gate[e]) * (xr @ w_up[e])
        return h @ w_down[e]
    yb = lax.map(expert_block, (xb, blk_e)).reshape(P, D)
    y = jnp.zeros((T, D), F32).at[tok_pad].add(yb.astype(F32) * gate_pad[:, None])
    return y.astype(x.dtype).reshape(bsz, L, D)


def token_mixer_prompt(x, p):
    zq, zf, zi, zg, fq, fk, fv, ff = split_projection(x @ p['w_in'])
    hq, hk, hv, hlogf = hgrn2_inputs(zq, zf, zi, p['lb'])
    state, o = hgrn2_prompt(hq, hk, hv, hlogf)
    q, k, v, logf = fox_inputs(fq, fk, fv, ff, p['fbias'])
    y = jnp.concatenate([hgrn2_readout(o, zg, p['hg_g']), fox_prompt(q, k, v, logf)], axis=-1) @ p['w_out']
    return y, k, v, logf.astype(x.dtype), state.astype(x.dtype)


def token_mixer_sample(x, ck, cv, clogf, S0, p):
    zq, zf, zi, zg, fq, fk, fv, ff = split_projection(x @ p['w_in'])
    hq, hk, hv, hlogf = hgrn2_inputs(zq, zf, zi, p['lb'])
    state, o = hgrn2_chunk(S0.astype(F32), hq, hk, hv, hlogf)
    q, k, v, logf = fox_inputs(fq, fk, fv, ff, p['fbias'])
    y = jnp.concatenate([hgrn2_readout(o, zg, p['hg_g']),
                         fox_sample(q, k, v, logf, ck, cv, clogf)], axis=-1) @ p['w_out']
    return y, k, v, logf.astype(x.dtype), state.astype(x.dtype)


def post_mixer(x, mix, mk, mv, p):
    x = layer_norm(DN_ALPHA * x + mix, p['ln1_g'], p['ln1_b'])
    x = layer_norm(DN_ALPHA * x + memory_attend(x, mk, mv, p['xa_wq'], p['xa_wo']), p['ln2_g'], p['ln2_b'])
    return layer_norm(DN_ALPHA * x + hier_moe(x, p), p['ln3_g'], p['ln3_b'])


def setup_inputs(seed: int = 0) -> dict:
    key = jax.random.key(seed)
    ks = jax.random.split(key, 32)
    def nrm(k, shape, scale=1.0):
        return jax.random.normal(k, shape, F32) * scale
    return {
        'x_prompt': nrm(ks[0], (BATCH, SEQ, D_MODEL)),
        'x_sample': nrm(ks[1], (DEC_BATCH, DEC_SEQ, D_MODEL)),
        'cache_fox_k': nrm(ks[2], (DEPTH, DEC_BATCH, PAST_LEN, FX_HEADS, FX_DIM)),
        'cache_fox_v': nrm(ks[3], (DEPTH, DEC_BATCH, PAST_LEN, FX_HEADS, FX_DIM)),
        'cache_fox_logf': jax.nn.log_sigmoid(3.0 + nrm(ks[4], (DEPTH, DEC_BATCH, PAST_LEN, FX_HEADS))),
        'state_hgrn': nrm(ks[5], (DEPTH, DEC_BATCH, HG_HEADS, HG_DIM, HG_DIM), 0.3),
        'cache_mem_k': nrm(ks[6], (DEPTH, DEC_BATCH, MEM_LEN, XA_HEADS, XA_DIM)),
        'cache_mem_v': nrm(ks[7], (DEPTH, DEC_BATCH, MEM_LEN, XA_HEADS, XA_DIM)),
        'mem_prompt': nrm(ks[8], (BATCH, MEM_LEN, D_MODEL)),
        'w_in': nrm(ks[9], (DEPTH, D_MODEL, IN_COLS), D_MODEL ** -0.5),
        'fox_fbias': 3.0 + nrm(ks[10], (DEPTH, FX_HEADS), 0.1),
        'hgrn_lb': nrm(ks[11], (DEPTH + 1, HG_WIDTH), 0.1),
        'hgrn_norm_g': 1.0 + nrm(ks[12], (DEPTH, HG_WIDTH), 0.02),
        'w_out': nrm(ks[13], (DEPTH, MIX_WIDTH, D_MODEL), MIX_WIDTH ** -0.5 * DN_BETA),
        'ln1_g': 1.0 + nrm(ks[14], (DEPTH, D_MODEL), 0.02),
        'ln1_b': nrm(ks[15], (DEPTH, D_MODEL), 0.02),
        'xa_wq': nrm(ks[16], (DEPTH, D_MODEL, D_MODEL), D_MODEL ** -0.5),
        'xa_wk': nrm(ks[17], (DEPTH, D_MODEL, D_MODEL), D_MODEL ** -0.5),
        'xa_wv': nrm(ks[18], (DEPTH, D_MODEL, D_MODEL), D_MODEL ** -0.5),
        'xa_wo': nrm(ks[19], (DEPTH, D_MODEL, D_MODEL), D_MODEL ** -0.5 * DN_BETA),
        'ln2_g': 1.0 + nrm(ks[20], (DEPTH, D_MODEL), 0.02),
        'ln2_b': nrm(ks[21], (DEPTH, D_MODEL), 0.02),
        'moe_w_group': nrm(ks[22], (DEPTH, D_MODEL, N_GROUPS), D_MODEL ** -0.5),
        'moe_b_group': nrm(ks[23], (DEPTH, N_GROUPS), 0.01),
        'moe_w_expert': nrm(ks[24], (DEPTH, D_MODEL, N_EXPERTS), D_MODEL ** -0.5),
        'moe_b_expert': nrm(ks[25], (DEPTH, N_EXPERTS), 0.01),
        'moe_w_gate': nrm(ks[26], (DEPTH, N_EXPERTS, D_MODEL, D_EXPERT), D_MODEL ** -0.5),
        'moe_w_up': nrm(ks[27], (DEPTH, N_EXPERTS, D_MODEL, D_EXPERT), D_MODEL ** -0.5),
        'moe_w_down': nrm(ks[28], (DEPTH, N_EXPERTS, D_EXPERT, D_MODEL), D_EXPERT ** -0.5 * DN_BETA),
        'ln3_g': 1.0 + nrm(ks[29], (DEPTH, D_MODEL), 0.02),
        'ln3_b': nrm(ks[30], (DEPTH, D_MODEL), 0.02),
    }


def reference(x_prompt, x_sample, cache_fox_k, cache_fox_v, cache_fox_logf, state_hgrn,
              cache_mem_k, cache_mem_v, mem_prompt, w_in, fox_fbias, hgrn_lb, hgrn_norm_g,
              w_out, ln1_g, ln1_b, xa_wq, xa_wk, xa_wv, xa_wo, ln2_g, ln2_b, moe_w_group,
              moe_b_group, moe_w_expert, moe_b_expert, moe_w_gate, moe_w_up, moe_w_down,
              ln3_g, ln3_b):
    lb_all = jnp.cumsum(jax.nn.softmax(hgrn_lb.astype(F32), axis=0), axis=0)
    xp, xs = x_prompt, x_sample
    fkp, fvp, flp, hsp, mkp_l, mvp_l = [], [], [], [], [], []
    fks, fvs, fls, hss = [], [], [], []
    for l in range(DEPTH):
        p = {'w_in': w_in[l], 'fbias': fox_fbias[l], 'lb': lb_all[l], 'hg_g': hgrn_norm_g[l],
             'w_out': w_out[l], 'ln1_g': ln1_g[l], 'ln1_b': ln1_b[l], 'xa_wq': xa_wq[l],
             'xa_wo': xa_wo[l], 'ln2_g': ln2_g[l], 'ln2_b': ln2_b[l],
             'w_group': moe_w_group[l], 'b_group': moe_b_group[l], 'w_expert': moe_w_expert[l],
             'b_expert': moe_b_expert[l], 'w_gate': moe_w_gate[l], 'w_up': moe_w_up[l],
             'w_down': moe_w_down[l], 'ln3_g': ln3_g[l], 'ln3_b': ln3_b[l]}
        mix_p, kp, vp, lfp, sp = token_mixer_prompt(xp, p)
        mkp, mvp = memory_kv(mem_prompt, xa_wk[l], xa_wv[l])
        xp = post_mixer(xp, mix_p, mkp, mvp, p)
        mix_s, ks_, vs_, lfs, ss = token_mixer_sample(xs, cache_fox_k[l], cache_fox_v[l],
                                                      cache_fox_logf[l], state_hgrn[l], p)
        xs = post_mixer(xs, mix_s, cache_mem_k[l], cache_mem_v[l], p)
        fkp.append(kp); fvp.append(vp); flp.append(lfp); hsp.append(sp)
        mkp_l.append(mkp); mvp_l.append(mvp)
        fks.append(ks_); fvs.append(vs_); fls.append(lfs); hss.append(ss)
    return (xp, xs, jnp.stack(fkp), jnp.stack(fvp), jnp.stack(flp), jnp.stack(hsp),
            jnp.stack(mkp_l), jnp.stack(mvp_l), jnp.stack(fks), jnp.stack(fvs), jnp.stack(fls),
            jnp.stack(hss))
```

```python
import functools
import math

import jax
import jax.numpy as jnp
import numpy as np
from jax import lax
from jax.experimental import pallas as pl
from jax.experimental.pallas import tpu as pltpu

F32 = jnp.float32
BF16 = jnp.bfloat16

LANES = 128
HG_HEADS = 4
HG_DIM = 128
HG_WIDTH = HG_HEADS * HG_DIM
FX_HEADS = 8
FX_DIM = 64
FX_WIDTH = FX_HEADS * FX_DIM
XA_HEADS = 4
N_GROUPS = 4
EXPERTS_PER_GROUP = 8
N_EXPERTS = N_GROUPS * EXPERTS_PER_GROUP
TOP_K = 2
LN_EPS = 1e-5
RMS_EPS = 1e-6
HG_CHUNK = 128
ROW_BLOCK = 256
ROUTE_LANE0 = N_GROUPS
VMEM_LIMIT = 56 * 1024 * 1024
NEG_INF = float("-inf")


def _mm(a, b):
    return jnp.dot(a, b, preferred_element_type=F32)


def _nt(a, b):
    return lax.dot_general(a, b, (((1,), (1,)), ((), ())), preferred_element_type=F32)


def _params(sem):
    return pltpu.CompilerParams(dimension_semantics=sem, vmem_limit_bytes=VMEM_LIMIT)


def _full(shape):
    nd = len(shape)
    return pl.BlockSpec(shape, lambda *_: (0,) * nd)


def _tile(n, pref):
    t = min(n, pref)
    assert n % t == 0, (n, t)
    return t


def _split3(x):
    a = x.astype(BF16)
    r = x - a.astype(F32)
    b = r.astype(BF16)
    c = (r - b.astype(F32)).astype(BF16)
    return a, b, c


def _layer_norm(x, g, b):
    xc = x - jnp.mean(x, axis=-1, keepdims=True)
    var = jnp.mean(xc * xc, axis=-1, keepdims=True)
    return xc * lax.rsqrt(var + LN_EPS) * g + b


def _proj_kernel(x_ref, whg_ref, wfx_ref, wff_ref, lb_ref, fb_ref,
                 zq_ref, hk_ref, hlf_ref, zi_ref, zg_ref, fq_ref, fk_ref, fv_ref, lf_ref):
    xb = x_ref[...].astype(BF16)
    w = HG_WIDTH
    zq_ref[...] = _mm(xb, whg_ref[:, 0:w])
    zf = _mm(xb, whg_ref[:, w:2 * w])
    lb = lb_ref[...]
    hk_ref[...] = (1.0 - lb) * jax.nn.sigmoid(-zf)
    hlf_ref[...] = jnp.log(lb + (1.0 - lb) * jax.nn.sigmoid(zf))
    zi_ref[...] = _mm(xb, whg_ref[:, 2 * w:3 * w])
    zg_ref[...] = _mm(xb, whg_ref[:, 3 * w:4 * w])
    f = FX_WIDTH
    fq_ref[...] = _mm(xb, wfx_ref[:, 0:f]).astype(BF16)
    fk_ref[...] = _mm(xb, wfx_ref[:, f:2 * f])
    fv_ref[...] = _mm(xb, wfx_ref[:, 2 * f:3 * f])
    z = _mm(xb, wff_ref[...]) + fb_ref[...]
    lf_ref[...] = jnp.minimum(z, 0.0) - jnp.log1p(jnp.exp(-jnp.abs(z)))


def _proj(x2d, whg, wfx, wff, lb, fb):
    t, d = x2d.shape
    tm = _tile(t, 256)
    row = lambda n: pl.BlockSpec((tm, n), lambda i: (i, 0))
    f32o = lambda n: jax.ShapeDtypeStruct((t, n), F32)
    return pl.pallas_call(
        _proj_kernel,
        grid=(t // tm,),
        in_specs=[row(d), _full(whg.shape), _full(wfx.shape), _full(wff.shape), _full(lb.shape), _full(fb.shape)],
        out_specs=[row(HG_WIDTH)] * 5 + [row(FX_WIDTH)] * 3 + [row(LANES)],
        out_shape=[f32o(HG_WIDTH)] * 5 + [jax.ShapeDtypeStruct((t, FX_WIDTH), BF16), f32o(FX_WIDTH), f32o(FX_WIDTH),
                                           f32o(LANES)],
        compiler_params=_params(("arbitrary",)),
        name="proj",
    )(x2d, whg, wfx, wff, lb, fb)


@functools.lru_cache(maxsize=None)
def _hgrn_consts(c):
    nlev = int(math.log2(c))
    assert 1 << nlev == c
    r = np.arange(c)[:, None]
    u = np.arange(c)[None, :]
    blocks = [u <= r, u > r]
    for lev in range(nlev):
        h = 1 << lev
        m = (r // (2 * h)) * (2 * h) + h
        blocks.append(np.where(r >= m, (u > m) & (u <= r), (u > r) & (u <= m)))
    gall = np.concatenate(blocks, axis=0).astype(np.float32)
    x = np.maximum(r ^ u, 1)
    lmap = np.where(r == u, -1, np.where(r > u, np.floor(np.log2(x)).astype(np.int64), -2)).astype(np.int32)
    return gall, lmap


def _hgrn_kernel(zq_ref, hk_ref, hlf_ref, zi_ref, zg_ref, gain_ref, s0_ref, gall_ref, lmap_ref,
                 y_ref, sout_ref, st_scr, *, c, nchunk):
    si = pl.program_id(1)

    @pl.when(si == 0)
    def _():
        for h in range(HG_HEADS):
            st_scr[h] = s0_ref[0, h].T

    nlev = int(math.log2(c))
    gall = gall_ref[...]
    lmap = lmap_ref[...]
    for h in range(HG_HEADS):
        cs = slice(h * HG_DIM, (h + 1) * HG_DIM)

        def chunk(ci, st, cs=cs):
            rows = pl.ds(pl.multiple_of(ci * c, c), c)
            q = zq_ref[0, rows, cs]
            k = hk_ref[0, rows, cs]
            v = zi_ref[0, rows, cs]
            lf = hlf_ref[0, rows, cs]
            zg = zg_ref[0, rows, cs]
            lf_hi = lf.astype(BF16)
            lf_lo = (lf - lf_hi.astype(F32)).astype(BF16)
            g2 = _mm(gall, jnp.concatenate([lf_hi, lf_lo], axis=1))
            g = g2[:, :HG_DIM] + g2[:, HG_DIM:]
            b = g[0:c]
            rem = g[c:2 * c]
            qb, kb, vb = q.astype(BF16), k.astype(BF16), v.astype(BF16)
            scores = jnp.where(lmap == -1, _nt(qb, kb), 0.0)
            for lev in range(nlev):
                fac = jnp.exp(g[(2 + lev) * c:(3 + lev) * c])
                p = _nt((q * fac).astype(BF16), (k * fac).astype(BF16))
                scores = jnp.where(lmap == lev, p, scores)
            o = _mm(scores.astype(BF16), vb) + _nt((q * jnp.exp(b)).astype(BF16), st.astype(BF16))
            kt = (k * jnp.exp(rem)).astype(BF16)
            st_new = st * jnp.exp(b[c - 1:c, :]) + _mm(v.T.astype(BF16), kt)
            ms = jnp.mean(o * o, axis=-1, keepdims=True)
            y = o * lax.rsqrt(ms + RMS_EPS) * gain_ref[:, cs] * (zg * jax.nn.sigmoid(zg))
            y_ref[0, rows, cs] = y.astype(BF16)
            return st_new

        st_scr[h] = lax.fori_loop(0, nchunk, chunk, st_scr[h])

    @pl.when(si == pl.num_programs(1) - 1)
    def _():
        for h in range(HG_HEADS):
            sout_ref[0, h] = st_scr[h].T


def _hgrn(zq, hk, hlf, zi, zg, gain, s0):
    bsz, s, w = zq.shape
    c = HG_CHUNK
    sb = _tile(s, 4 * c)
    gall, lmap = _hgrn_consts(c)
    gall = jnp.asarray(gall, BF16)
    lmap = jnp.asarray(lmap)
    blk = pl.BlockSpec((1, sb, w), lambda b, i: (b, i, 0))
    st_spec = pl.BlockSpec((1, HG_HEADS, HG_DIM, HG_DIM), lambda b, i: (b, 0, 0, 0))
    return pl.pallas_call(
        functools.partial(_hgrn_kernel, c=c, nchunk=sb // c),
        grid=(bsz, s // sb),
        in_specs=[blk] * 5 + [_full(gain.shape), st_spec, _full(gall.shape), _full(lmap.shape)],
        out_specs=[blk, st_spec],
        out_shape=[jax.ShapeDtypeStruct((bsz, s, w), BF16), jax.ShapeDtypeStruct(s0.shape, F32)],
        scratch_shapes=[pltpu.VMEM((HG_HEADS, HG_DIM, HG_DIM), F32)],
        compiler_params=_params(("arbitrary", "arbitrary")),
        name="hgrn",
    )(zq, hk, hlf, zi, zg, gain, s0, gall, lmap)


@functools.lru_cache(maxsize=None)
def _place_consts():
    pk = np.zeros((FX_HEADS, 3 * LANES, LANES), np.float32)
    pq = np.zeros((FX_HEADS, 3 * LANES, LANES), np.float32)
    for h in range(FX_HEADS):
        for j in range(3):
            pq[h, j * LANES + h, FX_DIM + j] = 1.0
            pk[h, j * LANES + h, FX_DIM + 3 + j] = -1.0
    return pq, pk


def _head_lanes(ref, h):
    p = h // 2
    x = ref[0, :, p * LANES:(p + 1) * LANES].astype(F32)
    return pltpu.roll(x, FX_DIM, axis=1) if h % 2 else x


def _prep_kv_kernel(fk_ref, fv_ref, lf_ref, pk_ref, ka_ref, va_ref, d_ref, carry, *, ts):
    @pl.when(pl.program_id(1) == 0)
    def _():
        carry[...] = jnp.zeros_like(carry)

    row = lax.broadcasted_iota(jnp.int32, (ts, ts), 0)
    col = lax.broadcasted_iota(jnp.int32, (ts, ts), 1)
    tri = jnp.where(col <= row, 1.0, 0.0).astype(BF16)
    c3 = _mm(tri, jnp.concatenate(_split3(lf_ref[0]), axis=1))
    d = c3[:, :LANES] + c3[:, LANES:2 * LANES] + c3[:, 2 * LANES:] + carry[...]
    carry[...] = d[ts - 1:ts, :]
    d_ref[0] = d
    dsplit = jnp.concatenate(_split3(d), axis=1)
    lane = lax.broadcasted_iota(jnp.int32, (ts, LANES), 1)
    ones_k = jnp.where((lane >= FX_DIM) & (lane < FX_DIM + 3), 1.0, 0.0)
    ones_v = jnp.where(lane == FX_DIM, 1.0, 0.0)
    for h in range(FX_HEADS):
        ek = _mm(dsplit, pk_ref[h]) + ones_k
        ka_ref[0, h] = jnp.where(lane < FX_DIM, _head_lanes(fk_ref, h), ek).astype(BF16)
        va_ref[0, h] = jnp.where(lane < FX_DIM, _head_lanes(fv_ref, h), ones_v).astype(BF16)


def _prep_kv(fk, fv, lf):
    bsz, s, w = fk.shape
    ts = _tile(s, 512)
    pk = jnp.asarray(_place_consts()[1], BF16)
    blk = lambda n: pl.BlockSpec((1, ts, n), lambda b, i: (b, i, 0))
    hblk = pl.BlockSpec((1, FX_HEADS, ts, LANES), lambda b, i: (b, 0, i, 0))
    hshape = jax.ShapeDtypeStruct((bsz, FX_HEADS, s, LANES), BF16)
    return pl.pallas_call(
        functools.partial(_prep_kv_kernel, ts=ts),
        grid=(bsz, s // ts),
        in_specs=[blk(w), blk(w), blk(LANES), _full(pk.shape)],
        out_specs=[hblk, hblk, blk(LANES)],
        out_shape=[hshape, hshape, jax.ShapeDtypeStruct((bsz, s, LANES), F32)],
        scratch_shapes=[pltpu.VMEM((1, LANES), F32)],
        compiler_params=_params(("arbitrary", "arbitrary")),
        name="fox_prep_kv",
    )(fk, fv, lf, pk)


def _prep_q_kernel(fq_ref, d_ref, pq_ref, qa_ref, *, ts):
    dsplit = jnp.concatenate(_split3(d_ref[0]), axis=1)
    lane = lax.broadcasted_iota(jnp.int32, (ts, LANES), 1)
    ones_q = jnp.where((lane >= FX_DIM + 3) & (lane < FX_DIM + 6), 1.0, 0.0)
    for h in range(FX_HEADS):
        eq = _mm(dsplit, pq_ref[h]) + ones_q
        qa_ref[0, h] = jnp.where(lane < FX_DIM, _head_lanes(fq_ref, h) * (FX_DIM ** -0.5), eq).astype(BF16)


def _prep_q(fq, dq):
    bsz, s, w = fq.shape
    ts = _tile(s, 512)
    pq = jnp.asarray(_place_consts()[0], BF16)
    blk = lambda n: pl.BlockSpec((1, ts, n), lambda b, i: (b, i, 0))
    return pl.pallas_call(
        functools.partial(_prep_q_kernel, ts=ts),
        grid=(bsz, s // ts),
        in_specs=[blk(w), blk(LANES), _full(pq.shape)],
        out_specs=pl.BlockSpec((1, FX_HEADS, ts, LANES), lambda b, i: (b, 0, i, 0)),
        out_shape=jax.ShapeDtypeStruct((bsz, FX_HEADS, s, LANES), BF16),
        compiler_params=_params(("arbitrary", "arbitrary")),
        name="fox_prep_q",
    )(fq, dq, pq)


def _flash_kernel(qa_ref, ka_ref, va_ref, o_ref, m_scr, acc_scr, *, tq, tk, q_off):
    i = pl.program_id(2)
    j = pl.program_id(3)

    @pl.when(j == 0)
    def _():
        m_scr[...] = jnp.full_like(m_scr, NEG_INF)
        acc_scr[...] = jnp.zeros_like(acc_scr)

    q_lo = q_off + i * tq
    j_last = (q_lo + tq - 1) // tk
    unmasked = j * tk + tk - 1 <= q_lo

    def step(masked):
        s = _nt(qa_ref[0, 0], ka_ref[0, 0])
        if masked:
            qpos = q_lo + lax.broadcasted_iota(jnp.int32, (tq, tk), 0)
            kpos = j * tk + lax.broadcasted_iota(jnp.int32, (tq, tk), 1)
            s = jnp.where(kpos <= qpos, s, NEG_INF)
        m_prev = m_scr[...]
        m_new = jnp.maximum(m_prev, jnp.max(s, axis=1, keepdims=True))
        p = jnp.exp(s - m_new)
        acc_scr[...] = jnp.exp(m_prev - m_new) * acc_scr[...] + _mm(p.astype(BF16), va_ref[0, 0])
        m_scr[...] = m_new

    @pl.when(jnp.logical_and(j <= j_last, unmasked))
    def _():
        step(False)

    @pl.when(jnp.logical_and(j <= j_last, jnp.logical_not(unmasked)))
    def _():
        step(True)

    @pl.when(j == pl.num_programs(3) - 1)
    def _():
        acc = acc_scr[...]
        o_ref[0] = (acc / acc[:, FX_DIM:FX_DIM + 1]).astype(BF16)


def _flash(qa, ka, va, q_off):
    bsz, nh, sq, _ = qa.shape
    sk = ka.shape[2]
    tq = _tile(sq, 512)
    tk = _tile(sk, 512)
    kv_idx = lambda b, h, i, j: (b, h, jnp.minimum(j, (q_off + i * tq + tq - 1) // tk), 0)
    return pl.pallas_call(
        functools.partial(_flash_kernel, tq=tq, tk=tk, q_off=q_off),
        grid=(bsz, nh, sq // tq, sk // tk),
        in_specs=[pl.BlockSpec((1, 1, tq, LANES), lambda b, h, i, j: (b, h, i, 0)),
                  pl.BlockSpec((1, 1, tk, LANES), kv_idx),
                  pl.BlockSpec((1, 1, tk, LANES), kv_idx)],
        out_specs=pl.BlockSpec((1, tq, LANES), lambda b, h, i, j: (b, i, h)),
        out_shape=jax.ShapeDtypeStruct((bsz, sq, nh * LANES), BF16),
        scratch_shapes=[pltpu.VMEM((tq, 1), F32), pltpu.VMEM((tq, LANES), F32)],
        compiler_params=_params(("arbitrary",) * 4),
        name="fox_flash",
    )(qa, ka, va)


def _mm2_kernel(x_ref, w_ref, o1_ref, o2_ref):
    n = o1_ref.shape[1]
    xb = x_ref[...].astype(BF16)
    o1_ref[...] = _mm(xb, w_ref[:, :n])
    o2_ref[...] = _mm(xb, w_ref[:, n:])


def _mm2(x2d, w):
    t, k = x2d.shape
    n = w.shape[1] // 2
    tm = _tile(t, 256)
    return pl.pallas_call(
        _mm2_kernel,
        grid=(t // tm,),
        in_specs=[pl.BlockSpec((tm, k), lambda i: (i, 0)), _full(w.shape)],
        out_specs=[pl.BlockSpec((tm, n), lambda i: (i, 0))] * 2,
        out_shape=[jax.ShapeDtypeStruct((t, n), F32)] * 2,
        compiler_params=_params(("arbitrary",)),
        name="mem_kv",
    )(x2d, w)


def _post1_kernel(x_ref, yhg_ref, yfx_ref, wohg_ref, wofx_ref, ln1g_ref, ln1b_ref, wq_ref, wxo_ref,
                  mk_ref, mv_ref, ln2g_ref, ln2b_ref, wrh_ref, wrl_ref, br_ref,
                  x2_ref, route_ref, cnt_ref, carry, *, tm, alpha):
    @pl.when(jnp.logical_and(pl.program_id(0) == 0, pl.program_id(1) == 0))
    def _():
        carry[...] = jnp.zeros_like(carry)

    mix = _mm(yhg_ref[0], wohg_ref[...]) + _mm(yfx_ref[0], wofx_ref[...])
    x1 = _layer_norm(alpha * x_ref[0] + mix, ln1g_ref[...], ln1b_ref[...])
    q = _mm(x1.astype(BF16), wq_ref[...])
    xa_dim = q.shape[1] // XA_HEADS
    heads = []
    for h in range(XA_HEADS):
        cs = slice(h * xa_dim, (h + 1) * xa_dim)
        s = _nt(q[:, cs].astype(BF16), mk_ref[0, :, cs].astype(BF16)) * (xa_dim ** -0.5)
        e = jnp.exp(s - jnp.max(s, axis=1, keepdims=True))
        p = e / jnp.sum(e, axis=1, keepdims=True)
        heads.append(_mm(p.astype(BF16), mv_ref[0, :, cs].astype(BF16)))
    xa = _mm(jnp.concatenate(heads, axis=1).astype(BF16), wxo_ref[...])
    x2 = _layer_norm(alpha * x1 + xa, ln2g_ref[...], ln2b_ref[...])
    x2_ref[0] = x2

    xh = x2.astype(BF16)
    xl = (x2 - xh.astype(F32)).astype(BF16)
    lg = _mm(xh, wrh_ref[...]) + _mm(xl, wrh_ref[...]) + _mm(xh, wrl_ref[...]) + br_ref[...]
    lane = lax.broadcasted_iota(jnp.int32, (tm, LANES), 1).astype(F32)
    big = float(LANES)
    gl = jnp.where(lane < N_GROUPS, lg, NEG_INF)
    gmax = jnp.max(gl, axis=1, keepdims=True)
    gsel = jnp.min(jnp.where(gl == gmax, lane, big), axis=1, keepdims=True)
    pg = 1.0 / jnp.sum(jnp.exp(gl - gmax), axis=1, keepdims=True)
    lo = ROUTE_LANE0 + EXPERTS_PER_GROUP * gsel
    el = jnp.where((lane >= lo) & (lane < lo + EXPERTS_PER_GROUP), lg, NEG_INF)
    v1 = jnp.max(el, axis=1, keepdims=True)
    i1 = jnp.min(jnp.where(el == v1, lane, big), axis=1, keepdims=True)
    el2 = jnp.where(lane == i1, NEG_INF, el)
    v2 = jnp.max(el2, axis=1, keepdims=True)
    i2 = jnp.min(jnp.where(el2 == v2, lane, big), axis=1, keepdims=True)
    e2 = jnp.exp(v2 - v1)
    g1 = pg / (1.0 + e2)
    g2 = pg * e2 / (1.0 + e2)

    hot = jnp.where((lane == i1) | (lane == i2), 1.0, 0.0)
    row = lax.broadcasted_iota(jnp.int32, (tm, tm), 0)
    col = lax.broadcasted_iota(jnp.int32, (tm, tm), 1)
    below = jnp.where(col < row, 1.0, 0.0).astype(BF16)
    cnt = _mm(below, hot.astype(BF16)) + carry[...]
    r1 = jnp.sum(jnp.where(lane == i1, cnt, 0.0), axis=1, keepdims=True)
    r2 = jnp.sum(jnp.where(lane == i2, cnt, 0.0), axis=1, keepdims=True)
    carry[...] = carry[...] + jnp.sum(hot, axis=0, keepdims=True)
    cnt_ref[...] = carry[...]
    route = jnp.where(lane == 0, i1 - ROUTE_LANE0, 0.0)
    for idx, val in ((1, i2 - ROUTE_LANE0), (2, g1), (3, g2), (4, r1), (5, r2)):
        route = jnp.where(lane == idx, val, route)
    route_ref[0] = route


def _post1(x, yhg, yfx, mk, mv, w, alpha):
    bsz, s, d = x.shape
    m = mk.shape[1]
    tm = _tile(s, 256)
    blk = lambda n: pl.BlockSpec((1, tm, n), lambda b, i: (b, i, 0))
    mem = pl.BlockSpec((1, m, d), lambda b, i: (b, 0, 0))
    names = ("wo_hg", "wo_fx", "ln1_g", "ln1_b", "xa_wq", "xa_wo")
    names2 = ("ln2_g", "ln2_b", "wr_hi", "wr_lo", "br")
    return pl.pallas_call(
        functools.partial(_post1_kernel, tm=tm, alpha=alpha),
        grid=(bsz, s // tm),
        in_specs=([blk(d), blk(yhg.shape[2]), blk(yfx.shape[2])] + [_full(w[n].shape) for n in names]
                  + [mem, mem] + [_full(w[n].shape) for n in names2]),
        out_specs=[blk(d), blk(LANES), _full((1, LANES))],
        out_shape=[jax.ShapeDtypeStruct((bsz, s, d), F32), jax.ShapeDtypeStruct((bsz, s, LANES), F32),
                   jax.ShapeDtypeStruct((1, LANES), F32)],
        scratch_shapes=[pltpu.VMEM((1, LANES), F32)],
        compiler_params=_params(("arbitrary", "arbitrary")),
        name="post1",
    )(x, yhg, yfx, *[w[n] for n in names], mk, mv, *[w[n] for n in names2])


def _row_copy(src_ref, src_row, dst_ref, dst_row, sem):
    return pltpu.make_async_copy(src_ref.at[pl.ds(src_row, 1)], dst_ref.at[pl.ds(dst_row, 1)], sem)


def _dispatch_kernel(dest_ref, x_ref, xs_in_ref, xs_ref, sem, *, tm):
    del xs_in_ref
    base = pl.program_id(0) * tm

    def start(r, _):
        for k in range(TOP_K):
            _row_copy(x_ref, base + r, xs_ref, dest_ref[0, 0, TOP_K * r + k], sem).start()
        return 0

    def wait(r, _):
        for k in range(TOP_K):
            _row_copy(x_ref, base + r, xs_ref, dest_ref[0, 0, TOP_K * r + k], sem).wait()
        return 0

    lax.fori_loop(0, tm, start, 0)
    lax.fori_loop(0, tm, wait, 0)


def _dispatch(x2d, dest3, p_rows):
    t, d = x2d.shape
    tm = dest3.shape[2] // TOP_K
    xs0 = jnp.zeros((p_rows, d), F32)
    return pl.pallas_call(
        functools.partial(_dispatch_kernel, tm=tm),
        grid=(t // tm,),
        in_specs=[pl.BlockSpec((1, 1, TOP_K * tm), lambda i: (i, 0, 0), memory_space=pltpu.SMEM),
                  pl.BlockSpec(memory_space=pl.ANY), pl.BlockSpec(memory_space=pl.ANY)],
        out_specs=pl.BlockSpec(memory_space=pl.ANY),
        out_shape=jax.ShapeDtypeStruct((p_rows, d), F32),
        scratch_shapes=[pltpu.SemaphoreType.DMA(())],
        input_output_aliases={2: 0},
        compiler_params=_params(("arbitrary",)),
        name="moe_dispatch",
    )(dest3, x2d, xs0)


def _expert_kernel(be_ref, nused_ref, x_ref, wg_ref, wu_ref, wd_ref, y_ref):
    del be_ref
    i = pl.program_id(0)

    @pl.when(i < nused_ref[0])
    def _():
        xb = x_ref[...].astype(BF16)
        a = _mm(xb, wg_ref[0])
        h = (a * jax.nn.sigmoid(a)) * _mm(xb, wu_ref[0])
        y_ref[...] = _mm(h.astype(BF16), wd_ref[0])

    @pl.when(i >= nused_ref[0])
    def _():
        y_ref[...] = jnp.zeros_like(y_ref)


def _experts(xs, blk_e, n_used, wg, wu, wd):
    p_rows, d = xs.shape
    de = wg.shape[2]
    nb = p_rows // ROW_BLOCK
    grid_spec = pltpu.PrefetchScalarGridSpec(
        num_scalar_prefetch=2,
        grid=(nb,),
        in_specs=[pl.BlockSpec((ROW_BLOCK, d), lambda i, be, nu: (i, 0)),
                  pl.BlockSpec((1, d, de), lambda i, be, nu: (be[i], 0, 0)),
                  pl.BlockSpec((1, d, de), lambda i, be, nu: (be[i], 0, 0)),
                  pl.BlockSpec((1, de, d), lambda i, be, nu: (be[i], 0, 0))],
        out_specs=pl.BlockSpec((ROW_BLOCK, d), lambda i, be, nu: (i, 0)),
    )
    return pl.pallas_call(
        _expert_kernel,
        grid_spec=grid_spec,
        out_shape=jax.ShapeDtypeStruct((p_rows, d), F32),
        compiler_params=_params(("arbitrary",)),
        name="moe_experts",
    )(blk_e, n_used, xs, wg, wu, wd)


def _combine_kernel(dest_ref, route_ref, x_ref, g_ref, b_ref, yb_ref, o_ref, buf, sem, *, tm, alpha):
    def start(r, _):
        for k in range(TOP_K):
            _row_copy(yb_ref, dest_ref[0, 0, TOP_K * r + k], buf.at[k], r, sem).start()
        return 0

    def wait(r, _):
        for k in range(TOP_K):
            _row_copy(yb_ref, dest_ref[0, 0, TOP_K * r + k], buf.at[k], r, sem).wait()
        return 0

    lax.fori_loop(0, tm, start, 0)
    lax.fori_loop(0, tm, wait, 0)
    route = route_ref[...]
    y = route[:, 2:3] * buf[0] + route[:, 3:4] * buf[1]
    o_ref[...] = _layer_norm(alpha * x_ref[...] + y, g_ref[...], b_ref[...])


def _combine(x2d, route2d, dest3, yb, g, b, alpha):
    t, d = x2d.shape
    tm = dest3.shape[2] // TOP_K
    return pl.pallas_call(
        functools.partial(_combine_kernel, tm=tm, alpha=alpha),
        grid=(t // tm,),
        in_specs=[pl.BlockSpec((1, 1, TOP_K * tm), lambda i: (i, 0, 0), memory_space=pltpu.SMEM),
                  pl.BlockSpec((tm, LANES), lambda i: (i, 0)),
                  pl.BlockSpec((tm, d), lambda i: (i, 0)),
                  _full(g.shape), _full(b.shape),
                  pl.BlockSpec(memory_space=pl.ANY)],
        out_specs=pl.BlockSpec((tm, d), lambda i: (i, 0)),
        out_shape=jax.ShapeDtypeStruct((t, d), F32),
        scratch_shapes=[pltpu.VMEM((TOP_K, tm, d), F32), pltpu.SemaphoreType.DMA(())],
        compiler_params=_params(("arbitrary",)),
        name="moe_combine",
    )(dest3, route2d, x2d, g, b, yb)


def _moe(x2d, route2d, counts, w, alpha):
    t, d = x2d.shape
    tm = _tile(t, 256)
    a = t * TOP_K
    n_blocks = -(-a // ROW_BLOCK) + N_EXPERTS
    cnt = counts[0, ROUTE_LANE0:ROUTE_LANE0 + N_EXPERTS].astype(jnp.int32)
    padded = (cnt + ROW_BLOCK - 1) // ROW_BLOCK * ROW_BLOCK
    pend = jnp.cumsum(padded)
    pstart = pend - padded
    eid = route2d[:, 0:TOP_K].astype(jnp.int32)
    rank = route2d[:, 4:4 + TOP_K].astype(jnp.int32)
    dest = pstart[eid] + rank
    dest3 = dest.reshape(t // tm, 1, TOP_K * tm)
    blk_e = jnp.minimum(jnp.searchsorted(pend, jnp.arange(n_blocks, dtype=jnp.int32) * ROW_BLOCK, side="right"),
                        N_EXPERTS - 1).astype(jnp.int32)
    n_used = (pend[-1:] // ROW_BLOCK).astype(jnp.int32)
    xs = _dispatch(x2d, dest3, n_blocks * ROW_BLOCK)
    yb = _experts(xs, blk_e, n_used, w["w_gate"], w["w_up"], w["w_down"])
    return _combine(x2d, route2d, dest3, yb, w["ln3_g"], w["ln3_b"], alpha)


def _layer_weights(l, depth, lb_all, w_in, fox_fbias, hgrn_norm_g, w_out, ln1_g, ln1_b, xa_wq, xa_wk, xa_wv, xa_wo,
                   ln2_g, ln2_b, moe_w_group, moe_b_group, moe_w_expert, moe_b_expert, moe_w_gate, moe_w_up,
                   moe_w_down, ln3_g, ln3_b):
    d = w_in.shape[1]
    row = lambda v: v.astype(F32).reshape(1, -1)
    pad_l = lambda v: jnp.pad(v, ((0, 0), (0, LANES - v.shape[1])))
    wi = w_in[l]
    n_hg = 4 * HG_WIDTH
    wo_fx = w_out[l][HG_WIDTH:].reshape(FX_HEADS, FX_DIM, d)
    wo_fx = jnp.pad(wo_fx, ((0, 0), (0, LANES - FX_DIM), (0, 0))).reshape(FX_HEADS * LANES, d)
    wr = pad_l(jnp.concatenate([moe_w_group[l], moe_w_expert[l]], axis=1).astype(F32))
    wr_hi = wr.astype(BF16)
    return {
        "whg": wi[:, :n_hg].astype(BF16),
        "wfx": wi[:, n_hg:n_hg + 3 * FX_WIDTH].astype(BF16),
        "wff": pad_l(wi[:, n_hg + 3 * FX_WIDTH:]).astype(BF16),
        "fb": pad_l(row(fox_fbias[l])),
        "lb": row(lb_all[l]),
        "gain": row(hgrn_norm_g[l]),
        "wo_hg": w_out[l][:HG_WIDTH].astype(BF16),
        "wo_fx": wo_fx.astype(BF16),
        "ln1_g": row(ln1_g[l]), "ln1_b": row(ln1_b[l]),
        "xa_wq": xa_wq[l].astype(BF16), "xa_wo": xa_wo[l].astype(BF16),
        "xa_wkv": jnp.concatenate([xa_wk[l], xa_wv[l]], axis=1).astype(BF16),
        "ln2_g": row(ln2_g[l]), "ln2_b": row(ln2_b[l]),
        "wr_hi": wr_hi, "wr_lo": (wr - wr_hi.astype(F32)).astype(BF16),
        "br": pad_l(jnp.concatenate([row(moe_b_group[l]), row(moe_b_expert[l])], axis=1)),
        "w_gate": moe_w_gate[l].astype(BF16), "w_up": moe_w_up[l].astype(BF16), "w_down": moe_w_down[l].astype(BF16),
        "ln3_g": row(ln3_g[l]), "ln3_b": row(ln3_b[l]),
    }


def _group_step(x, w, alpha, s0, mk, mv, cache_k=None, cache_v=None, cache_lf=None):
    bsz, s, d = x.shape
    zq, hk, hlf, zi, zg, fq, fk, fv, lf = _proj(x.reshape(bsz * s, d), w["whg"], w["wfx"], w["wff"], w["lb"], w["fb"])
    r3 = lambda v: v.reshape(bsz, s, v.shape[-1])

    s_pad = -(-s // HG_CHUNK) * HG_CHUNK
    padr = lambda v: jnp.pad(r3(v), ((0, 0), (0, s_pad - s), (0, 0)))
    yhg, state = _hgrn(padr(zq), padr(hk), padr(hlf), padr(zi), padr(zg), w["gain"], s0)
    yhg = yhg[:, :s]

    fq3, fk3, fv3, lf3 = r3(fq), r3(fk), r3(fv), r3(lf)
    if cache_k is None:
        ka, va, dcum = _prep_kv(fk3, fv3, lf3)
        qa = _prep_q(fq3, dcum)
        q_off = 0
    else:
        past = cache_k.shape[1]
        tail = -(past + s) % 512
        cat = lambda c, n: jnp.concatenate([c, n, jnp.zeros((bsz, tail, n.shape[2]), F32)], axis=1)
        lf_c = jnp.pad(cache_lf.astype(F32), ((0, 0), (0, 0), (0, LANES - cache_lf.shape[2])))
        ka, va, dcum = _prep_kv(cat(cache_k.reshape(bsz, past, FX_WIDTH).astype(F32), fk3),
                                cat(cache_v.reshape(bsz, past, FX_WIDTH).astype(F32), fv3), cat(lf_c, lf3))
        qa = _prep_q(fq3, dcum[:, past:past + s])
        q_off = past
    yfx = _flash(qa, ka, va, q_off)

    x2, route, counts = _post1(x, yhg, yfx, mk, mv, w, alpha)
    y = _moe(x2.reshape(bsz * s, d), route.reshape(bsz * s, LANES), counts, w, alpha)
    return (y.reshape(bsz, s, d), fk3.reshape(bsz, s, FX_HEADS, FX_DIM), fv3.reshape(bsz, s, FX_HEADS, FX_DIM),
            lf3[:, :, :FX_HEADS], state)


def kernel(x_prompt, x_sample, cache_fox_k, cache_fox_v, cache_fox_logf, state_hgrn, cache_mem_k, cache_mem_v, mem_prompt, w_in, fox_fbias, hgrn_lb, hgrn_norm_g, w_out, ln1_g, ln1_b, xa_wq, xa_wk, xa_wv, xa_wo, ln2_g, ln2_b, moe_w_group, moe_b_group, moe_w_expert, moe_b_expert, moe_w_gate, moe_w_up, moe_w_down, ln3_g, ln3_b):
    depth = w_in.shape[0]
    alpha = (2 * depth) ** 0.25
    lb_all = jnp.cumsum(jax.nn.softmax(hgrn_lb.astype(F32), axis=0), axis=0)
    bp, _, d = x_prompt.shape
    bs = x_sample.shape[0]
    m = mem_prompt.shape[1]
    xp, xs = x_prompt, x_sample
    outs = [[] for _ in range(10)]
    for l in range(depth):
        w = _layer_weights(l, depth, lb_all, w_in, fox_fbias, hgrn_norm_g, w_out, ln1_g, ln1_b, xa_wq, xa_wk, xa_wv,
                           xa_wo, ln2_g, ln2_b, moe_w_group, moe_b_group, moe_w_expert, moe_b_expert, moe_w_gate,
                           moe_w_up, moe_w_down, ln3_g, ln3_b)
        mkp, mvp = _mm2(mem_prompt.reshape(bp * m, d), w["xa_wkv"])
        mkp, mvp = mkp.reshape(bp, m, d), mvp.reshape(bp, m, d)
        s0p = jnp.zeros((bp, HG_HEADS, HG_DIM, HG_DIM), F32)
        xp, kp, vp, lfp, sp = _group_step(xp, w, alpha, s0p, mkp, mvp)
        xs, ks, vs, lfs, ss = _group_step(xs, w, alpha, state_hgrn[l].astype(F32),
                                          cache_mem_k[l].reshape(bs, m, d), cache_mem_v[l].reshape(bs, m, d),
                                          cache_fox_k[l], cache_fox_v[l], cache_fox_logf[l])
        hd = d // XA_HEADS
        for lst, v in zip(outs, (kp, vp, lfp, sp, mkp.reshape(bp, m, XA_HEADS, hd), mvp.reshape(bp, m, XA_HEADS, hd),
                                 ks, vs, lfs, ss)):
            lst.append(v)
    return (xp, xs) + tuple(jnp.stack(o) for o in outs)
```

```python
import functools
import math

import jax
import jax.numpy as jnp
import numpy as np
from jax import lax
from jax.experimental import pallas as pl
from jax.experimental.pallas import tpu as pltpu

F32 = jnp.float32
BF16 = jnp.bfloat16

LANES = 128
HG_HEADS = 4
HG_DIM = 128
HG_WIDTH = HG_HEADS * HG_DIM
FX_HEADS = 8
FX_DIM = 64
FX_WIDTH = FX_HEADS * FX_DIM
XA_HEADS = 4
N_GROUPS = 4
EXPERTS_PER_GROUP = 8
N_EXPERTS = N_GROUPS * EXPERTS_PER_GROUP
TOP_K = 2
LN_EPS = 1e-5
RMS_EPS = 1e-6
HG_CHUNK = 128
ROW_BLOCK = 256
ROUTE_LANE0 = N_GROUPS
VMEM_LIMIT = 56 * 1024 * 1024
NEG_INF = float("-inf")


def _mm(a, b):
    return jnp.dot(a, b, preferred_element_type=F32)


def _nt(a, b):
    return lax.dot_general(a, b, (((1,), (1,)), ((), ())), preferred_element_type=F32)


def _params(sem):
    return pltpu.CompilerParams(dimension_semantics=sem, vmem_limit_bytes=VMEM_LIMIT)


def _full(shape):
    nd = len(shape)
    return pl.BlockSpec(shape, lambda *_: (0,) * nd)


def _tile(n, pref):
    t = min(n, pref)
    assert n % t == 0, (n, t)
    return t


def _split3(x):
    a = x.astype(BF16)
    r = x - a.astype(F32)
    b = r.astype(BF16)
    c = (r - b.astype(F32)).astype(BF16)
    return a, b, c


def _layer_norm(x, g, b):
    xc = x - jnp.mean(x, axis=-1, keepdims=True)
    var = jnp.mean(xc * xc, axis=-1, keepdims=True)
    return xc * lax.rsqrt(var + LN_EPS) * g + b


def _proj_kernel(x_ref, whg_ref, wfx_ref, wff_ref, lb_ref, fb_ref,
                 zq_ref, hk_ref, hlf_ref, zi_ref, zg_ref, fq_ref, fk_ref, fv_ref, lf_ref):
    xb = x_ref[...].astype(BF16)
    w = HG_WIDTH
    zq_ref[...] = _mm(xb, whg_ref[:, 0:w])
    zf = _mm(xb, whg_ref[:, w:2 * w])
    lb = lb_ref[...]
    hk_ref[...] = (1.0 - lb) * jax.nn.sigmoid(-zf)
    hlf_ref[...] = jnp.log(lb + (1.0 - lb) * jax.nn.sigmoid(zf))
    zi_ref[...] = _mm(xb, whg_ref[:, 2 * w:3 * w])
    zg_ref[...] = _mm(xb, whg_ref[:, 3 * w:4 * w])
    f = FX_WIDTH
    fq_ref[...] = _mm(xb, wfx_ref[:, 0:f]).astype(BF16)
    fk_ref[...] = _mm(xb, wfx_ref[:, f:2 * f])
    fv_ref[...] = _mm(xb, wfx_ref[:, 2 * f:3 * f])
    z = _mm(xb, wff_ref[...]) + fb_ref[...]
    lf_ref[...] = jnp.minimum(z, 0.0) - jnp.log1p(jnp.exp(-jnp.abs(z)))


def _proj(x2d, whg, wfx, wff, lb, fb):
    t, d = x2d.shape
    tm = _tile(t, 256)
    row = lambda n: pl.BlockSpec((tm, n), lambda i: (i, 0))
    f32o = lambda n: jax.ShapeDtypeStruct((t, n), F32)
    return pl.pallas_call(
        _proj_kernel,
        grid=(t // tm,),
        in_specs=[row(d), _full(whg.shape), _full(wfx.shape), _full(wff.shape), _full(lb.shape), _full(fb.shape)],
        out_specs=[row(HG_WIDTH)] * 5 + [row(FX_WIDTH)] * 3 + [row(LANES)],
        out_shape=[f32o(HG_WIDTH)] * 5 + [jax.ShapeDtypeStruct((t, FX_WIDTH), BF16), f32o(FX_WIDTH), f32o(FX_WIDTH),
                                           f32o(LANES)],
        compiler_params=_params(("arbitrary",)),
        name="proj",
    )(x2d, whg, wfx, wff, lb, fb)


@functools.lru_cache(maxsize=None)
def _hgrn_consts(c):
    nlev = int(math.log2(c))
    assert 1 << nlev == c
    r = np.arange(c)[:, None]
    u = np.arange(c)[None, :]
    blocks = [u <= r, u > r]
    for lev in range(nlev):
        h = 1 << lev
        m = (r // (2 * h)) * (2 * h) + h
        blocks.append(np.where(r >= m, (u > m) & (u <= r), (u > r) & (u <= m)))
    gall = np.concatenate(blocks, axis=0).astype(np.float32)
    x = np.maximum(r ^ u, 1)
    lmap = np.where(r == u, -1, np.where(r > u, np.floor(np.log2(x)).astype(np.int64), -2)).astype(np.int32)
    return gall, lmap


def _hgrn_kernel(zq_ref, hk_ref, hlf_ref, zi_ref, zg_ref, gain_ref, s0_ref, gall_ref, lmap_ref,
                 y_ref, sout_ref, st_scr, *, c, nchunk):
    si = pl.program_id(1)

    @pl.when(si == 0)
    def _():
        for h in range(HG_HEADS):
            st_scr[h] = s0_ref[0, h].T

    nlev = int(math.log2(c))
    gall = gall_ref[...]
    lmap = lmap_ref[...]
    for h in range(HG_HEADS):
        cs = slice(h * HG_DIM, (h + 1) * HG_DIM)

        def chunk(ci, st, cs=cs):
            rows = pl.ds(pl.multiple_of(ci * c, c), c)
            q = zq_ref[0, rows, cs]
            k = hk_ref[0, rows, cs]
            v = zi_ref[0, rows, cs]
            lf = hlf_ref[0, rows, cs]
            zg = zg_ref[0, rows, cs]
            lf_hi = lf.astype(BF16)
            lf_lo = (lf - lf_hi.astype(F32)).astype(BF16)
            g2 = _mm(gall, jnp.concatenate([lf_hi, lf_lo], axis=1))
            g = g2[:, :HG_DIM] + g2[:, HG_DIM:]
            b = g[0:c]
            rem = g[c:2 * c]
            qb, kb, vb = q.astype(BF16), k.astype(BF16), v.astype(BF16)
            scores = jnp.where(lmap == -1, _nt(qb, kb), 0.0)
            for lev in range(nlev):
                fac = jnp.exp(g[(2 + lev) * c:(3 + lev) * c])
                p = _nt((q * fac).astype(BF16), (k * fac).astype(BF16))
                scores = jnp.where(lmap == lev, p, scores)
            o = _mm(scores.astype(BF16), vb) + _nt((q * jnp.exp(b)).astype(BF16), st.astype(BF16))
            kt = (k * jnp.exp(rem)).astype(BF16)
            st_new = st * jnp.exp(b[c - 1:c, :]) + _mm(v.T.astype(BF16), kt)
            ms = jnp.mean(o * o, axis=-1, keepdims=True)
            y = o * lax.rsqrt(ms + RMS_EPS) * gain_ref[:, cs] * (zg * jax.nn.sigmoid(zg))
            y_ref[0, rows, cs] = y.astype(BF16)
            return st_new

        st_scr[h] = lax.fori_loop(0, nchunk, chunk, st_scr[h])

    @pl.when(si == pl.num_programs(1) - 1)
    def _():
        for h in range(HG_HEADS):
            sout_ref[0, h] = st_scr[h].T


def _hgrn(zq, hk, hlf, zi, zg, gain, s0):
    bsz, s, w = zq.shape
    c = HG_CHUNK
    sb = _tile(s, 4 * c)
    gall, lmap = _hgrn_consts(c)
    gall = jnp.asarray(gall, BF16)
    lmap = jnp.asarray(lmap)
    blk = pl.BlockSpec((1, sb, w), lambda b, i: (b, i, 0))
    st_spec = pl.BlockSpec((1, HG_HEADS, HG_DIM, HG_DIM), lambda b, i: (b, 0, 0, 0))
    return pl.pallas_call(
        functools.partial(_hgrn_kernel, c=c, nchunk=sb // c),
        grid=(bsz, s // sb),
        in_specs=[blk] * 5 + [_full(gain.shape), st_spec, _full(gall.shape), _full(lmap.shape)],
        out_specs=[blk, st_spec],
        out_shape=[jax.ShapeDtypeStruct((bsz, s, w), BF16), jax.ShapeDtypeStruct(s0.shape, F32)],
        scratch_shapes=[pltpu.VMEM((HG_HEADS, HG_DIM, HG_DIM), F32)],
        compiler_params=_params(("arbitrary", "arbitrary")),
        name="hgrn",
    )(zq, hk, hlf, zi, zg, gain, s0, gall, lmap)


@functools.lru_cache(maxsize=None)
def _place_consts():
    pk = np.zeros((FX_HEADS, 3 * LANES, LANES), np.float32)
    pq = np.zeros((FX_HEADS, 3 * LANES, LANES), np.float32)
    for h in range(FX_HEADS):
        for j in range(3):
            pq[h, j * LANES + h, FX_DIM + j] = 1.0
            pk[h, j * LANES + h, FX_DIM + 3 + j] = -1.0
    return pq, pk


def _head_lanes(ref, h):
    p = h // 2
    x = ref[0, :, p * LANES:(p + 1) * LANES].astype(F32)
    return pltpu.roll(x, FX_DIM, axis=1) if h % 2 else x


def _row_stat(x):
    return jnp.broadcast_to(jnp.max(x, axis=0, keepdims=True), (1, LANES))


def _head_norm2(xa, lane):
    xf = xa.astype(F32)
    return _row_stat(jnp.sum(jnp.where(lane < FX_DIM, xf * xf, 0.0), axis=1, keepdims=True))


def _prep_kv_kernel(fk_ref, fv_ref, lf_ref, pk_ref, ka_ref, va_ref, d_ref, kst_ref, dst_ref, carry, *, ts):
    @pl.when(pl.program_id(1) == 0)
    def _():
        carry[...] = jnp.zeros_like(carry)

    row = lax.broadcasted_iota(jnp.int32, (ts, ts), 0)
    col = lax.broadcasted_iota(jnp.int32, (ts, ts), 1)
    tri = jnp.where(col <= row, 1.0, 0.0).astype(BF16)
    c3 = _mm(tri, jnp.concatenate(_split3(lf_ref[0]), axis=1))
    d = c3[:, :LANES] + c3[:, LANES:2 * LANES] + c3[:, 2 * LANES:] + carry[...]
    carry[...] = d[ts - 1:ts, :]
    d_ref[0] = d
    dst_ref[0, 0] = jnp.zeros((8, LANES), F32)
    dst_ref[0, 0, 0:1, :] = jnp.max(d, axis=0, keepdims=True)
    dst_ref[0, 0, 1:2, :] = jnp.min(d, axis=0, keepdims=True)
    dsplit = jnp.concatenate(_split3(d), axis=1)
    lane = lax.broadcasted_iota(jnp.int32, (ts, LANES), 1)
    ones_k = jnp.where((lane >= FX_DIM) & (lane < FX_DIM + 3), 1.0, 0.0)
    ones_v = jnp.where(lane == FX_DIM, 1.0, 0.0)
    for h in range(FX_HEADS):
        ek = _mm(dsplit, pk_ref[h]) + ones_k
        ka = jnp.where(lane < FX_DIM, _head_lanes(fk_ref, h), ek).astype(BF16)
        ka_ref[0, h] = ka
        kst_ref[0, 0, h:h + 1, :] = _head_norm2(ka, lane)
        va_ref[0, h] = jnp.where(lane < FX_DIM, _head_lanes(fv_ref, h), ones_v).astype(BF16)


def _prep_kv(fk, fv, lf):
    bsz, s, w = fk.shape
    ts = _tile(s, 512)
    pk = jnp.asarray(_place_consts()[1], BF16)
    blk = lambda n: pl.BlockSpec((1, ts, n), lambda b, i: (b, i, 0))
    hblk = pl.BlockSpec((1, FX_HEADS, ts, LANES), lambda b, i: (b, 0, i, 0))
    sblk = pl.BlockSpec((1, 1, 8, LANES), lambda b, i: (b, i, 0, 0))
    hshape = jax.ShapeDtypeStruct((bsz, FX_HEADS, s, LANES), BF16)
    sshape = jax.ShapeDtypeStruct((bsz, s // ts, 8, LANES), F32)
    return pl.pallas_call(
        functools.partial(_prep_kv_kernel, ts=ts),
        grid=(bsz, s // ts),
        in_specs=[blk(w), blk(w), blk(LANES), _full(pk.shape)],
        out_specs=[hblk, hblk, blk(LANES), sblk, sblk],
        out_shape=[hshape, hshape, jax.ShapeDtypeStruct((bsz, s, LANES), F32), sshape, sshape],
        scratch_shapes=[pltpu.VMEM((1, LANES), F32)],
        compiler_params=_params(("arbitrary", "arbitrary")),
        name="fox_prep_kv",
    )(fk, fv, lf, pk)


def _prep_q_kernel(fq_ref, d_ref, pq_ref, qa_ref, qst_ref, dqs_ref, *, ts):
    d = d_ref[0]
    dqs_ref[0, 0] = jnp.broadcast_to(jnp.max(d, axis=0, keepdims=True), (8, LANES))
    dsplit = jnp.concatenate(_split3(d), axis=1)
    lane = lax.broadcasted_iota(jnp.int32, (ts, LANES), 1)
    ones_q = jnp.where((lane >= FX_DIM + 3) & (lane < FX_DIM + 6), 1.0, 0.0)
    for h in range(FX_HEADS):
        eq = _mm(dsplit, pq_ref[h]) + ones_q
        qa = jnp.where(lane < FX_DIM, _head_lanes(fq_ref, h) * (FX_DIM ** -0.5), eq).astype(BF16)
        qa_ref[0, h] = qa
        qst_ref[0, 0, h:h + 1, :] = _head_norm2(qa, lane)


def _prep_q(fq, dq):
    bsz, s, w = fq.shape
    ts = _tile(s, 512)
    pq = jnp.asarray(_place_consts()[0], BF16)
    blk = lambda n: pl.BlockSpec((1, ts, n), lambda b, i: (b, i, 0))
    sblk = pl.BlockSpec((1, 1, 8, LANES), lambda b, i: (b, i, 0, 0))
    sshape = jax.ShapeDtypeStruct((bsz, s // ts, 8, LANES), F32)
    return pl.pallas_call(
        functools.partial(_prep_q_kernel, ts=ts),
        grid=(bsz, s // ts),
        in_specs=[blk(w), blk(LANES), _full(pq.shape)],
        out_specs=[pl.BlockSpec((1, FX_HEADS, ts, LANES), lambda b, i: (b, 0, i, 0)), sblk, sblk],
        out_shape=[jax.ShapeDtypeStruct((bsz, FX_HEADS, s, LANES), BF16), sshape, sshape],
        compiler_params=_params(("arbitrary", "arbitrary")),
        name="fox_prep_q",
    )(fq, dq, pq)


EXP_ZERO_GAP = 110.0


def _skip_table(qst, dqs, kst, dst, tq, tk, q_off):
    nq, nk = qst.shape[1], kst.shape[1]
    qn = 1.02 * jnp.sqrt(qst[:, :, :, 0])
    kn = jnp.sqrt(kst[:, :, :, 0])
    dq_max = dqs[:, :, 0, :FX_HEADS]
    dk_min = dst[:, :, 1, :FX_HEADS]
    n_full = q_off // tk + (np.arange(nq) if tq == tk else np.zeros(nq, np.int64))
    ub = qn[:, :, None, :] * kn[:, None, :, :] + dq_max[:, :, None, :] - dk_min[:, None, :, :]
    row_max_lb = -qn * kn[:, n_full, :]
    skip = (ub < row_max_lb[:, :, None, :] - EXP_ZERO_GAP) & (np.arange(nk)[None, None, :, None] < n_full[None, :, None, None])
    js = jnp.sum(jnp.cumprod(skip.astype(jnp.int32), axis=2), axis=2)
    return jnp.transpose(js, (0, 2, 1)).reshape(-1).astype(jnp.int32)


def _flash_kernel(js_ref, qa_ref, ka_ref, va_ref, o_ref, m_scr, acc_scr, *, tq, tk, q_off):
    b, h, i = pl.program_id(0), pl.program_id(1), pl.program_id(2)
    nh, nq = pl.num_programs(1), pl.num_programs(2)
    n_full = q_off // tk + (i if tq == tk else 0)
    q_lo = q_off + i * tq
    m_scr[...] = jnp.full_like(m_scr, NEG_INF)
    acc_scr[...] = jnp.zeros_like(acc_scr)

    def block(j, masked):
        rows = pl.ds(pl.multiple_of(j * tk, tk), tk)
        s = _nt(qa_ref[0, 0], ka_ref[0, 0, rows, :])
        if masked:
            qpos = q_lo + lax.broadcasted_iota(jnp.int32, (tq, tk), 0)
            kpos = j * tk + lax.broadcasted_iota(jnp.int32, (tq, tk), 1)
            s = jnp.where(kpos <= qpos, s, NEG_INF)
        m_prev = m_scr[...]
        m_new = jnp.maximum(m_prev, jnp.max(s, axis=1, keepdims=True))
        p = jnp.exp(s - m_new)
        acc_scr[...] = jnp.exp(m_prev - m_new) * acc_scr[...] + _mm(p.astype(BF16), va_ref[0, 0, rows, :])
        m_scr[...] = m_new

    def body(j, c):
        block(j, False)
        return c

    lax.fori_loop(js_ref[(b * nh + h) * nq + i], n_full, body, 0)
    block(n_full, True)
    acc = acc_scr[...]
    o_ref[0] = (acc / acc[:, FX_DIM:FX_DIM + 1]).astype(BF16)


def _flash(qa, ka, va, js, q_off):
    bsz, nh, sq, _ = qa.shape
    sk = ka.shape[2]
    tq = _tile(sq, 512)
    tk = _tile(sk, 512)
    assert q_off % tk == 0 and (tq == tk or sq == tq) and tq <= tk
    kv = pl.BlockSpec((1, 1, sk, LANES), lambda b, h, i, js: (b, h, 0, 0))
    grid_spec = pltpu.PrefetchScalarGridSpec(
        num_scalar_prefetch=1,
        grid=(bsz, nh, sq // tq),
        in_specs=[pl.BlockSpec((1, 1, tq, LANES), lambda b, h, i, js: (b, h, i, 0)), kv, kv],
        out_specs=pl.BlockSpec((1, tq, LANES), lambda b, h, i, js: (b, i, h)),
        scratch_shapes=[pltpu.VMEM((tq, 1), F32), pltpu.VMEM((tq, LANES), F32)],
    )
    return pl.pallas_call(
        functools.partial(_flash_kernel, tq=tq, tk=tk, q_off=q_off),
        grid_spec=grid_spec,
        out_shape=jax.ShapeDtypeStruct((bsz, sq, nh * LANES), BF16),
        compiler_params=_params(("arbitrary",) * 3),
        name="fox_flash",
    )(js, qa, ka, va)


def _mm2_kernel(x_ref, w_ref, o1_ref, o2_ref):
    n = o1_ref.shape[1]
    xb = x_ref[...].astype(BF16)
    o1_ref[...] = _mm(xb, w_ref[:, :n])
    o2_ref[...] = _mm(xb, w_ref[:, n:])


def _mm2(x2d, w):
    t, k = x2d.shape
    n = w.shape[1] // 2
    tm = _tile(t, 256)
    return pl.pallas_call(
        _mm2_kernel,
        grid=(t // tm,),
        in_specs=[pl.BlockSpec((tm, k), lambda i: (i, 0)), _full(w.shape)],
        out_specs=[pl.BlockSpec((tm, n), lambda i: (i, 0))] * 2,
        out_shape=[jax.ShapeDtypeStruct((t, n), F32)] * 2,
        compiler_params=_params(("arbitrary",)),
        name="mem_kv",
    )(x2d, w)


def _post1_kernel(x_ref, yhg_ref, yfx_ref, wohg_ref, wofx_ref, ln1g_ref, ln1b_ref, wq_ref, wxo_ref,
                  mk_ref, mv_ref, ln2g_ref, ln2b_ref, wrh_ref, wrl_ref, br_ref,
                  x2_ref, route_ref, cnt_ref, carry, *, tm, alpha):
    @pl.when(jnp.logical_and(pl.program_id(0) == 0, pl.program_id(1) == 0))
    def _():
        carry[...] = jnp.zeros_like(carry)

    mix = _mm(yhg_ref[0], wohg_ref[...]) + _mm(yfx_ref[0], wofx_ref[...])
    x1 = _layer_norm(alpha * x_ref[0] + mix, ln1g_ref[...], ln1b_ref[...])
    q = _mm(x1.astype(BF16), wq_ref[...])
    xa_dim = q.shape[1] // XA_HEADS
    heads = []
    for h in range(XA_HEADS):
        cs = slice(h * xa_dim, (h + 1) * xa_dim)
        s = _nt(q[:, cs].astype(BF16), mk_ref[0, :, cs].astype(BF16)) * (xa_dim ** -0.5)
        e = jnp.exp(s - jnp.max(s, axis=1, keepdims=True))
        p = e / jnp.sum(e, axis=1, keepdims=True)
        heads.append(_mm(p.astype(BF16), mv_ref[0, :, cs].astype(BF16)))
    xa = _mm(jnp.concatenate(heads, axis=1).astype(BF16), wxo_ref[...])
    x2 = _layer_norm(alpha * x1 + xa, ln2g_ref[...], ln2b_ref[...])
    x2_ref[0] = x2

    xh = x2.astype(BF16)
    xl = (x2 - xh.astype(F32)).astype(BF16)
    lg = _mm(xh, wrh_ref[...]) + _mm(xl, wrh_ref[...]) + _mm(xh, wrl_ref[...]) + br_ref[...]
    lane = lax.broadcasted_iota(jnp.int32, (tm, LANES), 1).astype(F32)
    big = float(LANES)
    gl = jnp.where(lane < N_GROUPS, lg, NEG_INF)
    gmax = jnp.max(gl, axis=1, keepdims=True)
    gsel = jnp.min(jnp.where(gl == gmax, lane, big), axis=1, keepdims=True)
    pg = 1.0 / jnp.sum(jnp.exp(gl - gmax), axis=1, keepdims=True)
    lo = ROUTE_LANE0 + EXPERTS_PER_GROUP * gsel
    el = jnp.where((lane >= lo) & (lane < lo + EXPERTS_PER_GROUP), lg, NEG_INF)
    v1 = jnp.max(el, axis=1, keepdims=True)
    i1 = jnp.min(jnp.where(el == v1, lane, big), axis=1, keepdims=True)
    el2 = jnp.where(lane == i1, NEG_INF, el)
    v2 = jnp.max(el2, axis=1, keepdims=True)
    i2 = jnp.min(jnp.where(el2 == v2, lane, big), axis=1, keepdims=True)
    e2 = jnp.exp(v2 - v1)
    g1 = pg / (1.0 + e2)
    g2 = pg * e2 / (1.0 + e2)

    hot = jnp.where((lane == i1) | (lane == i2), 1.0, 0.0)
    row = lax.broadcasted_iota(jnp.int32, (tm, tm), 0)
    col = lax.broadcasted_iota(jnp.int32, (tm, tm), 1)
    below = jnp.where(col < row, 1.0, 0.0).astype(BF16)
    cnt = _mm(below, hot.astype(BF16)) + carry[...]
    r1 = jnp.sum(jnp.where(lane == i1, cnt, 0.0), axis=1, keepdims=True)
    r2 = jnp.sum(jnp.where(lane == i2, cnt, 0.0), axis=1, keepdims=True)
    carry[...] = carry[...] + jnp.sum(hot, axis=0, keepdims=True)
    cnt_ref[...] = carry[...]
    route = jnp.where(lane == 0, i1 - ROUTE_LANE0, 0.0)
    for idx, val in ((1, i2 - ROUTE_LANE0), (2, g1), (3, g2), (4, r1), (5, r2)):
        route = jnp.where(lane == idx, val, route)
    route_ref[0] = route


def _post1(x, yhg, yfx, mk, mv, w, alpha):
    bsz, s, d = x.shape
    m = mk.shape[1]
    tm = _tile(s, 256)
    blk = lambda n: pl.BlockSpec((1, tm, n), lambda b, i: (b, i, 0))
    mem = pl.BlockSpec((1, m, d), lambda b, i: (b, 0, 0))
    names = ("wo_hg", "wo_fx", "ln1_g", "ln1_b", "xa_wq", "xa_wo")
    names2 = ("ln2_g", "ln2_b", "wr_hi", "wr_lo", "br")
    return pl.pallas_call(
        functools.partial(_post1_kernel, tm=tm, alpha=alpha),
        grid=(bsz, s // tm),
        in_specs=([blk(d), blk(yhg.shape[2]), blk(yfx.shape[2])] + [_full(w[n].shape) for n in names]
                  + [mem, mem] + [_full(w[n].shape) for n in names2]),
        out_specs=[blk(d), blk(LANES), _full((1, LANES))],
        out_shape=[jax.ShapeDtypeStruct((bsz, s, d), F32), jax.ShapeDtypeStruct((bsz, s, LANES), F32),
                   jax.ShapeDtypeStruct((1, LANES), F32)],
        scratch_shapes=[pltpu.VMEM((1, LANES), F32)],
        compiler_params=_params(("arbitrary", "arbitrary")),
        name="post1",
    )(x, yhg, yfx, *[w[n] for n in names], mk, mv, *[w[n] for n in names2])


def _row_copy(src_ref, src_row, dst_ref, dst_row, sem):
    return pltpu.make_async_copy(src_ref.at[pl.ds(src_row, 1)], dst_ref.at[pl.ds(dst_row, 1)], sem)


def _dispatch_kernel(dest_ref, x_ref, xs_in_ref, xs_ref, sem, *, tm):
    del xs_in_ref

    def start(r, _):
        for k in range(TOP_K):
            _row_copy(x_ref, r, xs_ref, dest_ref[0, 0, TOP_K * r + k], sem).start()
        return 0

    def wait(r, _):
        for k in range(TOP_K):
            _row_copy(x_ref, r, xs_ref, dest_ref[0, 0, TOP_K * r + k], sem).wait()
        return 0

    lax.fori_loop(0, tm, start, 0)
    lax.fori_loop(0, tm, wait, 0)


def _dispatch(x2d, dest3, p_rows):
    t, d = x2d.shape
    tm = dest3.shape[2] // TOP_K
    xs0 = jnp.zeros((p_rows, d), F32)
    return pl.pallas_call(
        functools.partial(_dispatch_kernel, tm=tm),
        grid=(t // tm,),
        in_specs=[pl.BlockSpec((1, 1, TOP_K * tm), lambda i: (i, 0, 0), memory_space=pltpu.SMEM),
                  pl.BlockSpec((tm, d), lambda i: (i, 0)), pl.BlockSpec(memory_space=pl.ANY)],
        out_specs=pl.BlockSpec(memory_space=pl.ANY),
        out_shape=jax.ShapeDtypeStruct((p_rows, d), F32),
        scratch_shapes=[pltpu.SemaphoreType.DMA(())],
        input_output_aliases={2: 0},
        compiler_params=_params(("arbitrary",)),
        name="moe_dispatch",
    )(dest3, x2d, xs0)


def _expert_kernel(be_ref, nused_ref, x_ref, wg_ref, wu_ref, wd_ref, y_ref):
    del be_ref
    i = pl.program_id(0)

    @pl.when(i < nused_ref[0])
    def _():
        xb = x_ref[...].astype(BF16)
        a = _mm(xb, wg_ref[0])
        h = (a * jax.nn.sigmoid(a)) * _mm(xb, wu_ref[0])
        y_ref[...] = _mm(h.astype(BF16), wd_ref[0])

    @pl.when(i >= nused_ref[0])
    def _():
        y_ref[...] = jnp.zeros_like(y_ref)


def _experts(xs, blk_e, n_used, wg, wu, wd):
    p_rows, d = xs.shape
    de = wg.shape[2]
    nb = p_rows // ROW_BLOCK
    grid_spec = pltpu.PrefetchScalarGridSpec(
        num_scalar_prefetch=2,
        grid=(nb,),
        in_specs=[pl.BlockSpec((ROW_BLOCK, d), lambda i, be, nu: (i, 0)),
                  pl.BlockSpec((1, d, de), lambda i, be, nu: (be[i], 0, 0)),
                  pl.BlockSpec((1, d, de), lambda i, be, nu: (be[i], 0, 0)),
                  pl.BlockSpec((1, de, d), lambda i, be, nu: (be[i], 0, 0))],
        out_specs=pl.BlockSpec((ROW_BLOCK, d), lambda i, be, nu: (i, 0)),
    )
    return pl.pallas_call(
        _expert_kernel,
        grid_spec=grid_spec,
        out_shape=jax.ShapeDtypeStruct((p_rows, d), F32),
        compiler_params=_params(("arbitrary",)),
        name="moe_experts",
    )(blk_e, n_used, xs, wg, wu, wd)


def _combine_kernel(dest_ref, route_ref, x_ref, g_ref, b_ref, yb_ref, o_ref, buf, sem, *, tm, alpha):
    def start(r, _):
        for k in range(TOP_K):
            _row_copy(yb_ref, dest_ref[0, 0, TOP_K * r + k], buf.at[k], r, sem).start()
        return 0

    def wait(r, _):
        for k in range(TOP_K):
            _row_copy(yb_ref, dest_ref[0, 0, TOP_K * r + k], buf.at[k], r, sem).wait()
        return 0

    lax.fori_loop(0, tm, start, 0)
    lax.fori_loop(0, tm, wait, 0)
    route = route_ref[...]
    y = route[:, 2:3] * buf[0] + route[:, 3:4] * buf[1]
    o_ref[...] = _layer_norm(alpha * x_ref[...] + y, g_ref[...], b_ref[...])


def _combine(x2d, route2d, dest3, yb, g, b, alpha):
    t, d = x2d.shape
    tm = dest3.shape[2] // TOP_K
    return pl.pallas_call(
        functools.partial(_combine_kernel, tm=tm, alpha=alpha),
        grid=(t // tm,),
        in_specs=[pl.BlockSpec((1, 1, TOP_K * tm), lambda i: (i, 0, 0), memory_space=pltpu.SMEM),
                  pl.BlockSpec((tm, LANES), lambda i: (i, 0)),
                  pl.BlockSpec((tm, d), lambda i: (i, 0)),
                  _full(g.shape), _full(b.shape),
                  pl.BlockSpec(memory_space=pl.ANY)],
        out_specs=pl.BlockSpec((tm, d), lambda i: (i, 0)),
        out_shape=jax.ShapeDtypeStruct((t, d), F32),
        scratch_shapes=[pltpu.VMEM((TOP_K, tm, d), F32), pltpu.SemaphoreType.DMA(())],
        compiler_params=_params(("arbitrary",)),
        name="moe_combine",
    )(dest3, route2d, x2d, g, b, yb)


def _moe(x2d, route2d, counts, w, alpha):
    t, d = x2d.shape
    tm = _tile(t, 256)
    a = t * TOP_K
    n_blocks = -(-a // ROW_BLOCK) + N_EXPERTS
    cnt = counts[0, ROUTE_LANE0:ROUTE_LANE0 + N_EXPERTS].astype(jnp.int32)
    padded = (cnt + ROW_BLOCK - 1) // ROW_BLOCK * ROW_BLOCK
    pend = jnp.cumsum(padded)
    pstart = pend - padded
    eid = route2d[:, 0:TOP_K].astype(jnp.int32)
    rank = route2d[:, 4:4 + TOP_K].astype(jnp.int32)
    dest = pstart[eid] + rank
    dest3 = dest.reshape(t // tm, 1, TOP_K * tm)
    blk_lo = jnp.arange(n_blocks, dtype=jnp.int32) * ROW_BLOCK
    blk_e = jnp.minimum(jnp.sum((pend[None, :] <= blk_lo[:, None]).astype(jnp.int32), axis=1), N_EXPERTS - 1)
    n_used = (pend[-1:] // ROW_BLOCK).astype(jnp.int32)
    xs = _dispatch(x2d, dest3, n_blocks * ROW_BLOCK)
    yb = _experts(xs, blk_e, n_used, w["w_gate"], w["w_up"], w["w_down"])
    return _combine(x2d, route2d, dest3, yb, w["ln3_g"], w["ln3_b"], alpha)


def _layer_weights(l, depth, lb_all, w_in, fox_fbias, hgrn_norm_g, w_out, ln1_g, ln1_b, xa_wq, xa_wk, xa_wv, xa_wo,
                   ln2_g, ln2_b, moe_w_group, moe_b_group, moe_w_expert, moe_b_expert, moe_w_gate, moe_w_up,
                   moe_w_down, ln3_g, ln3_b):
    d = w_in.shape[1]
    row = lambda v: v.astype(F32).reshape(1, -1)
    pad_l = lambda v: jnp.pad(v, ((0, 0), (0, LANES - v.shape[1])))
    wi = w_in[l]
    n_hg = 4 * HG_WIDTH
    wo_fx = w_out[l][HG_WIDTH:].reshape(FX_HEADS, FX_DIM, d)
    wo_fx = jnp.pad(wo_fx, ((0, 0), (0, LANES - FX_DIM), (0, 0))).reshape(FX_HEADS * LANES, d)
    wr = pad_l(jnp.concatenate([moe_w_group[l], moe_w_expert[l]], axis=1).astype(F32))
    wr_hi = wr.astype(BF16)
    return {
        "whg": wi[:, :n_hg].astype(BF16),
        "wfx": wi[:, n_hg:n_hg + 3 * FX_WIDTH].astype(BF16),
        "wff": pad_l(wi[:, n_hg + 3 * FX_WIDTH:]).astype(BF16),
        "fb": pad_l(row(fox_fbias[l])),
        "lb": row(lb_all[l]),
        "gain": row(hgrn_norm_g[l]),
        "wo_hg": w_out[l][:HG_WIDTH].astype(BF16),
        "wo_fx": wo_fx.astype(BF16),
        "ln1_g": row(ln1_g[l]), "ln1_b": row(ln1_b[l]),
        "xa_wq": xa_wq[l].astype(BF16), "xa_wo": xa_wo[l].astype(BF16),
        "xa_wkv": jnp.concatenate([xa_wk[l], xa_wv[l]], axis=1).astype(BF16),
        "ln2_g": row(ln2_g[l]), "ln2_b": row(ln2_b[l]),
        "wr_hi": wr_hi, "wr_lo": (wr - wr_hi.astype(F32)).astype(BF16),
        "br": pad_l(jnp.concatenate([row(moe_b_group[l]), row(moe_b_expert[l])], axis=1)),
        "w_gate": moe_w_gate[l].astype(BF16), "w_up": moe_w_up[l].astype(BF16), "w_down": moe_w_down[l].astype(BF16),
        "ln3_g": row(ln3_g[l]), "ln3_b": row(ln3_b[l]),
    }


def _group_step(x, w, alpha, s0, mk, mv, cache_k=None, cache_v=None, cache_lf=None):
    bsz, s, d = x.shape
    zq, hk, hlf, zi, zg, fq, fk, fv, lf = _proj(x.reshape(bsz * s, d), w["whg"], w["wfx"], w["wff"], w["lb"], w["fb"])
    r3 = lambda v: v.reshape(bsz, s, v.shape[-1])

    s_pad = -(-s // HG_CHUNK) * HG_CHUNK
    padr = lambda v: jnp.pad(r3(v), ((0, 0), (0, s_pad - s), (0, 0)))
    yhg, state = _hgrn(padr(zq), padr(hk), padr(hlf), padr(zi), padr(zg), w["gain"], s0)
    yhg = yhg[:, :s]

    fq3, fk3, fv3, lf3 = r3(fq), r3(fk), r3(fv), r3(lf)
    if cache_k is None:
        ka, va, dcum, kst, dst = _prep_kv(fk3, fv3, lf3)
        qa, qst, dqs = _prep_q(fq3, dcum)
        q_off = 0
    else:
        past = cache_k.shape[1]
        tail = -(past + s) % 512
        cat = lambda c, n: jnp.concatenate([c, n, jnp.zeros((bsz, tail, n.shape[2]), F32)], axis=1)
        lf_c = jnp.pad(cache_lf.astype(F32), ((0, 0), (0, 0), (0, LANES - cache_lf.shape[2])))
        ka, va, dcum, kst, dst = _prep_kv(cat(cache_k.reshape(bsz, past, FX_WIDTH).astype(F32), fk3),
                                          cat(cache_v.reshape(bsz, past, FX_WIDTH).astype(F32), fv3), cat(lf_c, lf3))
        qa, qst, dqs = _prep_q(fq3, dcum[:, past:past + s])
        q_off = past
    tq, tk = _tile(s, 512), _tile(ka.shape[2], 512)
    yfx = _flash(qa, ka, va, _skip_table(qst, dqs, kst, dst, tq, tk, q_off), q_off)

    x2, route, counts = _post1(x, yhg, yfx, mk, mv, w, alpha)
    y = _moe(x2.reshape(bsz * s, d), route.reshape(bsz * s, LANES), counts, w, alpha)
    return (y.reshape(bsz, s, d), fk3.reshape(bsz, s, FX_HEADS, FX_DIM), fv3.reshape(bsz, s, FX_HEADS, FX_DIM),
            lf3[:, :, :FX_HEADS], state)


def kernel(x_prompt, x_sample, cache_fox_k, cache_fox_v, cache_fox_logf, state_hgrn, cache_mem_k, cache_mem_v, mem_prompt, w_in, fox_fbias, hgrn_lb, hgrn_norm_g, w_out, ln1_g, ln1_b, xa_wq, xa_wk, xa_wv, xa_wo, ln2_g, ln2_b, moe_w_group, moe_b_group, moe_w_expert, moe_b_expert, moe_w_gate, moe_w_up, moe_w_down, ln3_g, ln3_b):
    depth = w_in.shape[0]
    alpha = (2 * depth) ** 0.25
    lb_all = jnp.cumsum(jax.nn.softmax(hgrn_lb.astype(F32), axis=0), axis=0)
    bp, _, d = x_prompt.shape
    bs = x_sample.shape[0]
    m = mem_prompt.shape[1]
    xp, xs = x_prompt, x_sample
    outs = [[] for _ in range(10)]
    for l in range(depth):
        w = _layer_weights(l, depth, lb_all, w_in, fox_fbias, hgrn_norm_g, w_out, ln1_g, ln1_b, xa_wq, xa_wk, xa_wv,
                           xa_wo, ln2_g, ln2_b, moe_w_group, moe_b_group, moe_w_expert, moe_b_expert, moe_w_gate,
                           moe_w_up, moe_w_down, ln3_g, ln3_b)
        mkp, mvp = _mm2(mem_prompt.reshape(bp * m, d), w["xa_wkv"])
        mkp, mvp = mkp.reshape(bp, m, d), mvp.reshape(bp, m, d)
        s0p = jnp.zeros((bp, HG_HEADS, HG_DIM, HG_DIM), F32)
        xp, kp, vp, lfp, sp = _group_step(xp, w, alpha, s0p, mkp, mvp)
        xs, ks, vs, lfs, ss = _group_step(xs, w, alpha, state_hgrn[l].astype(F32),
                                          cache_mem_k[l].reshape(bs, m, d), cache_mem_v[l].reshape(bs, m, d),
                                          cache_fox_k[l], cache_fox_v[l], cache_fox_logf[l])
        hd = d // XA_HEADS
        for lst, v in zip(outs, (kp, vp, lfp, sp, mkp.reshape(bp, m, XA_HEADS, hd), mvp.reshape(bp, m, XA_HEADS, hd),
                                 ks, vs, lfs, ss)):
            lst.append(v)
    return (xp, xs) + tuple(jnp.stack(o) for o in outs)
```
